```python
import jax
import jax.numpy as jnp
from jax import lax
import numpy as np

D_MODEL = 4096
BATCH = 4
SEQ = 4096
DEPTH = 1
DEC_BATCH = 1
DEC_SEQ = 16384
PAST_LEN = 128

HEAD_DIM = 128
GRID_W = 64
NA_HEADS = 14
NA_WIDTH = NA_HEADS * HEAD_DIM
NA_KH = 8
NA_KW = 16
NA_QCOLS = 16
NA_KSPAN = 32
DIL_PATTERNS = ((128, 1), (512, 4), (2048, 16))
DIL_GROUP_HEADS = 6
DIL_HEADS = DIL_GROUP_HEADS * len(DIL_PATTERNS)
DIL_WIDTH = DIL_HEADS * HEAD_DIM
DIL_OUT_WIDTH = DIL_GROUP_HEADS * HEAD_DIM
ROPE_THETA = 500000.0
ROPE_DIMS = HEAD_DIM // 4
N_EXPERTS = 16
EXPERT_FF = D_MODEL
EC_CAPACITY_FACTOR = 2
LN_EPS = 1e-5
DEEPNORM_ALPHA = (2 * DEPTH) ** 0.25
DEEPNORM_BETA = (8 * DEPTH) ** -0.25
N_IN = 3 * NA_WIDTH + 3 * DIL_WIDTH + 2 * D_MODEL
NEG_INF = -1e30

kernel_name = 'hybrid_natten_dilated_ec_encoder'


def layer_norm(x, g, b):
    xf = x.astype(jnp.float32)
    mu = jnp.mean(xf, axis=-1, keepdims=True)
    xc = xf - mu
    var = jnp.mean(xc * xc, axis=-1, keepdims=True)
    y = xc * lax.rsqrt(var + LN_EPS) * g.astype(jnp.float32) + b.astype(jnp.float32)
    return y.astype(x.dtype)


def partial_rotary(x):
    s = x.shape[1]
    half = ROPE_DIMS // 2
    inv_freq = ROPE_THETA ** (-jnp.arange(half, dtype=jnp.float32) / half)
    ang = jnp.arange(s, dtype=jnp.float32)[:, None] * inv_freq[None, :]
    cos = jnp.cos(ang)[None, :, None, :]
    sin = jnp.sin(ang)[None, :, None, :]
    x1 = x[..., :half].astype(jnp.float32)
    x2 = x[..., half:ROPE_DIMS].astype(jnp.float32)
    rot = jnp.concatenate([x1 * cos - x2 * sin, x2 * cos + x1 * sin], axis=-1).astype(x.dtype)
    return jnp.concatenate([rot, x[..., ROPE_DIMS:]], axis=-1)


def _na_column_tables():
    nqc = GRID_W // NA_QCOLS
    qcols = np.arange(GRID_W).reshape(nqc, NA_QCOLS)
    span_start = np.clip(np.arange(nqc) * NA_QCOLS - NA_KW // 2, 0, GRID_W - NA_KSPAN)
    kcols = span_start[:, None] + np.arange(NA_KSPAN)[None, :]
    win_start = np.clip(qcols - NA_KW // 2, 0, GRID_W - NA_KW)
    valid = (kcols[:, None, :] >= win_start[:, :, None]) & (kcols[:, None, :] < win_start[:, :, None] + NA_KW)
    dc_idx = np.clip(kcols[:, None, :] - qcols[:, :, None] + NA_KW - 1, 0, 2 * NA_KW - 2)
    return kcols, valid, dc_idx


def neighborhood_attention(q, k, v, rel_bias):
    b, s, h, dh = q.shape
    rows = s // GRID_W
    kh = min(NA_KH, rows)
    kcols, valid, dc_idx = _na_column_tables()
    nqc = kcols.shape[0]
    qg = q.reshape(b, rows, nqc, NA_QCOLS, h, dh).transpose(1, 0, 2, 3, 4, 5)
    kg = k.reshape(b, rows, GRID_W, h, dh)[:, :, kcols]
    vg = v.reshape(b, rows, GRID_W, h, dh)[:, :, kcols]
    col_valid = jnp.asarray(valid)[:, :, None, :]
    bias_col = jnp.take(rel_bias, jnp.asarray(dc_idx), axis=2).astype(jnp.float32)
    scale = dh ** -0.5

    def row_block(args):
        q_r, r = args
        sr = jnp.clip(r - kh // 2, 0, rows - kh)
        k_band = lax.dynamic_slice_in_dim(kg, sr, kh, axis=1)
        v_band = lax.dynamic_slice_in_dim(vg, sr, kh, axis=1)
        sc = jnp.einsum('bcqhd,brckhd->bhcqrk', q_r, k_band, preferred_element_type=jnp.float32) * scale
        dr_idx = sr + jnp.arange(kh) - r + NA_KH - 1
        bias = jnp.take(bias_col, dr_idx, axis=1).transpose(0, 2, 3, 1, 4)
        sc = jnp.where(col_valid, sc + bias, NEG_INF)
        p = jax.nn.softmax(sc.reshape(sc.shape[:4] + (kh * NA_KSPAN,)), axis=-1).reshape(sc.shape)
        return jnp.einsum('bhcqrk,brckhd->bcqhd', p.astype(v.dtype), v_band)

    out = lax.map(row_block, (qg, jnp.arange(rows)))
    return out.transpose(1, 0, 2, 3, 4, 5).reshape(b, s, h * dh)


def banded_window_attention(q, k, v, radius):
    n, m, h, dh = q.shape
    qb = radius
    nb = -(-m // qb)
    mp = nb * qb
    qp = jnp.pad(q, ((0, 0), (0, mp - m), (0, 0), (0, 0))).reshape(n, nb, qb, h, dh)

    def band(x):
        xp = jnp.pad(x, ((0, 0), (qb, mp - m + qb), (0, 0), (0, 0))).reshape(n, nb + 2, qb, h, dh)
        return jnp.concatenate([xp[:, :-2], xp[:, 1:-1], xp[:, 2:]], axis=2)

    kb, vb = band(k), band(v)
    mq = jnp.arange(nb)[:, None] * qb + jnp.arange(qb)[None, :]
    mk = (jnp.arange(nb)[:, None] - 1) * qb + jnp.arange(3 * qb)[None, :]
    valid = (jnp.abs(mq[:, :, None] - mk[:, None, :]) <= radius) & (mk[:, None, :] >= 0) & (mk[:, None, :] < m)
    sc = jnp.einsum('nbqhd,nbkhd->nbhqk', qp, kb, preferred_element_type=jnp.float32) * (dh ** -0.5)
    sc = jnp.where(valid[:, None], sc, NEG_INF)
    mx = jnp.max(sc, axis=-1, keepdims=True)
    p = jnp.exp(sc - mx)
    den = jnp.sum(p, axis=-1, keepdims=True)
    o = jnp.einsum('nbhqk,nbkhd->nbqhd', (p / den).astype(v.dtype), vb)
    lse = (mx + jnp.log(den))[..., 0].transpose(0, 1, 3, 2)
    return o.reshape(n, mp, h, dh)[:, :m], lse.reshape(n, mp, h)[:, :m]


def dilated_group_attention(q, k, v, window, dilation):
    b, s, h, dh = q.shape
    m = s // dilation
    radius = window // (2 * dilation)

    def to_streams(x):
        return x.reshape(b, m, dilation, h, dh).transpose(0, 2, 1, 3, 4).reshape(b * dilation, m, h, dh)

    o, lse = banded_window_attention(to_streams(q), to_streams(k), to_streams(v), radius)
    o = o.reshape(b, dilation, m, h, dh).transpose(0, 2, 1, 3, 4).reshape(b, s, h, dh)
    lse = lse.reshape(b, dilation, m, h).transpose(0, 2, 1, 3).reshape(b, s, h)
    return o, lse


def dilated_attention(q, k, v):
    b, s, _, dh = q.shape
    outs, lses = [], []
    for g, (window, dilation) in enumerate(DIL_PATTERNS):
        hs = slice(g * DIL_GROUP_HEADS, (g + 1) * DIL_GROUP_HEADS)
        o, lse = dilated_group_attention(q[:, :, hs], k[:, :, hs], v[:, :, hs], window, dilation)
        outs.append(o)
        lses.append(lse)
    w = jax.nn.softmax(jnp.stack(lses, axis=0), axis=0)
    out = jnp.sum(w[..., None] * jnp.stack(outs, axis=0).astype(jnp.float32), axis=0)
    return out.astype(q.dtype).reshape(b, s, DIL_OUT_WIDTH)


def token_mixer(h, w_in, b_gate, na_rel_bias, w_branch_na, w_branch_dil, w_out):
    b, s, _ = h.shape
    proj = jnp.einsum('bsd,dn->bsn', h, w_in)
    cuts = [NA_WIDTH, 2 * NA_WIDTH, 3 * NA_WIDTH,
            3 * NA_WIDTH + DIL_WIDTH, 3 * NA_WIDTH + 2 * DIL_WIDTH, 3 * NA_WIDTH + 3 * DIL_WIDTH]
    q_na, k_na, v_na, q_dl, k_dl, v_dl, gate_logits = jnp.split(proj, cuts, axis=-1)
    y_na = neighborhood_attention(q_na.reshape(b, s, NA_HEADS, HEAD_DIM), k_na.reshape(b, s, NA_HEADS, HEAD_DIM),
                                  v_na.reshape(b, s, NA_HEADS, HEAD_DIM), na_rel_bias)
    y_dl = dilated_attention(partial_rotary(q_dl.reshape(b, s, DIL_HEADS, HEAD_DIM)),
                             partial_rotary(k_dl.reshape(b, s, DIL_HEADS, HEAD_DIM)),
                             v_dl.reshape(b, s, DIL_HEADS, HEAD_DIM))
    gates = jax.nn.sigmoid((gate_logits + b_gate).astype(jnp.float32)).astype(h.dtype)
    g_na, g_dl = gates[..., :D_MODEL], gates[..., D_MODEL:]
    merged = g_na * jnp.einsum('bsw,wd->bsd', y_na, w_branch_na) + g_dl * jnp.einsum('bsw,wd->bsd', y_dl, w_branch_dil)
    return jnp.einsum('bsd,de->bse', merged, w_out)


def expert_choice_ffn(x, w_router, w_exp_gate, w_exp_up, w_exp_down):
    b, s, d = x.shape
    t = b * s
    cap = (EC_CAPACITY_FACTOR * t) // N_EXPERTS
    xt = x.reshape(t, d)
    aff = jax.nn.softmax(jnp.einsum('td,de->te', xt, w_router, preferred_element_type=jnp.float32), axis=-1)
    gate, idx = lax.top_k(aff.T, cap)
    xe = xt[idx]
    hid = jax.nn.silu(jnp.einsum('ecd,edf->ecf', xe, w_exp_gate)) * jnp.einsum('ecd,edf->ecf', xe, w_exp_up)
    ye = jnp.einsum('ecf,efd->ecd', hid, w_exp_down) * gate[..., None].astype(x.dtype)
    y = jnp.zeros_like(xt).at[idx.reshape(-1)].add(ye.reshape(-1, d))
    return y.reshape(b, s, d)


def encoder(x, ln_emb_g, ln_emb_b, w_in, b_gate, na_rel_bias, w_branch_na, w_branch_dil, w_out,
            ln_mix_g, ln_mix_b, w_router, w_exp_gate, w_exp_up, w_exp_down, ln_moe_g, ln_moe_b):
    h = layer_norm(x, ln_emb_g, ln_emb_b)
    for l in range(DEPTH):
        mix = token_mixer(h, w_in[l], b_gate[l], na_rel_bias[l], w_branch_na[l], w_branch_dil[l], w_out[l])
        h = layer_norm(DEEPNORM_ALPHA * h + mix, ln_mix_g[l], ln_mix_b[l])
        ffn = expert_choice_ffn(h, w_router[l], w_exp_gate[l], w_exp_up[l], w_exp_down[l])
        h = layer_norm(DEEPNORM_ALPHA * h + ffn, ln_moe_g[l], ln_moe_b[l])
    return h


def setup_inputs(seed: int = 0) -> dict:
    key = jax.random.key(seed)
    ks = jax.random.split(key, 20)
    f32 = jnp.float32
    d_s = D_MODEL ** -0.5
    col_scale = jnp.concatenate([
        jnp.full((2 * NA_WIDTH,), d_s, f32), jnp.full((NA_WIDTH,), DEEPNORM_BETA * d_s, f32),
        jnp.full((2 * DIL_WIDTH,), d_s, f32), jnp.full((DIL_WIDTH,), DEEPNORM_BETA * d_s, f32),
        jnp.full((2 * D_MODEL,), d_s, f32)])
    return {
        'x_prompt': jax.random.normal(ks[0], (BATCH, SEQ, D_MODEL), f32),
        'x_sample': jax.random.normal(ks[1], (DEC_BATCH, DEC_SEQ, D_MODEL), f32),
        'ln_emb_g': 1.0 + 0.02 * jax.random.normal(ks[2], (D_MODEL,), f32),
        'ln_emb_b': 0.02 * jax.random.normal(ks[3], (D_MODEL,), f32),
        'w_in': jax.random.normal(ks[4], (DEPTH, D_MODEL, N_IN), f32) * col_scale,
        'b_gate': 0.02 * jax.random.normal(ks[5], (DEPTH, 2 * D_MODEL), f32),
        'na_rel_bias': 0.1 * jax.random.normal(ks[6], (DEPTH, NA_HEADS, 2 * NA_KH - 1, 2 * NA_KW - 1), f32),
        'w_branch_na': jax.random.normal(ks[7], (DEPTH, NA_WIDTH, D_MODEL), f32) * (DEEPNORM_BETA * NA_WIDTH ** -0.5),
        'w_branch_dil': jax.random.normal(ks[8], (DEPTH, DIL_OUT_WIDTH, D_MODEL), f32) * (DEEPNORM_BETA * DIL_OUT_WIDTH ** -0.5),
        'w_out': jax.random.normal(ks[9], (DEPTH, D_MODEL, D_MODEL), f32) * (DEEPNORM_BETA * d_s),
        'ln_mix_g': 1.0 + 0.02 * jax.random.normal(ks[10], (DEPTH, D_MODEL), f32),
        'ln_mix_b': 0.02 * jax.random.normal(ks[11], (DEPTH, D_MODEL), f32),
        'w_router': jax.random.normal(ks[12], (DEPTH, D_MODEL, N_EXPERTS), f32) * d_s,
        'w_exp_gate': jax.random.normal(ks[13], (DEPTH, N_EXPERTS, D_MODEL, EXPERT_FF), f32) * d_s,
        'w_exp_up': jax.random.normal(ks[14], (DEPTH, N_EXPERTS, D_MODEL, EXPERT_FF), f32) * (DEEPNORM_BETA * d_s),
        'w_exp_down': jax.random.normal(ks[15], (DEPTH, N_EXPERTS, EXPERT_FF, D_MODEL), f32) * (DEEPNORM_BETA * EXPERT_FF ** -0.5),
        'ln_moe_g': 1.0 + 0.02 * jax.random.normal(ks[16], (DEPTH, D_MODEL), f32),
        'ln_moe_b': 0.02 * jax.random.normal(ks[17], (DEPTH, D_MODEL), f32),
    }


def reference(x_prompt, x_sample, ln_emb_g, ln_emb_b, w_in, b_gate, na_rel_bias, w_branch_na, w_branch_dil,
              w_out, ln_mix_g, ln_mix_b, w_router, w_exp_gate, w_exp_up, w_exp_down, ln_moe_g, ln_moe_b):
    y_prompt = encoder(x_prompt, ln_emb_g, ln_emb_b, w_in, b_gate, na_rel_bias, w_branch_na, w_branch_dil, w_out,
                       ln_mix_g, ln_mix_b, w_router, w_exp_gate, w_exp_up, w_exp_down, ln_moe_g, ln_moe_b)
    y_sample = encoder(x_sample, ln_emb_g, ln_emb_b, w_in, b_gate, na_rel_bias, w_branch_na, w_branch_dil, w_out,
                       ln_mix_g, ln_mix_b, w_router, w_exp_gate, w_exp_up, w_exp_down, ln_moe_g, ln_moe_b)
    return (y_prompt, y_sample)
```

```python
import functools

import numpy as np
import jax
import jax.numpy as jnp
from jax import lax
from jax.experimental import pallas as pl
from jax.experimental.pallas import tpu as pltpu

HEAD_DIM = 128
GRID_W = 64
NA_HEADS = 14
NA_KH = 8
NA_KW = 16
DIL_PATTERNS = ((128, 1), (512, 4), (2048, 16))
DIL_GROUP_HEADS = 6
DIL_HEADS = DIL_GROUP_HEADS * len(DIL_PATTERNS)
ROPE_THETA = 500000.0
ROPE_DIMS = HEAD_DIM // 4
N_EXPERTS = 16
EC_CAPACITY_FACTOR = 2
LN_EPS = 1e-5
DEPTH = 1
DEEPNORM_ALPHA = (2 * DEPTH) ** 0.25
NEG_INF = -1e30

LANES = 128
VMEM_LIMIT_BYTES = 56 * 1024 * 1024

F32 = jnp.float32
BF16 = jnp.bfloat16
_NT = (((1,), (1,)), ((), ()))


def _params(n_grid_dims):
    return pltpu.CompilerParams(dimension_semantics=("arbitrary",) * n_grid_dims,
                                vmem_limit_bytes=VMEM_LIMIT_BYTES)


def _layer_norm_f32(x, g, b):
    mu = jnp.mean(x, axis=-1, keepdims=True)
    xc = x - mu
    var = jnp.mean(xc * xc, axis=-1, keepdims=True)
    return xc * lax.rsqrt(var + LN_EPS) * g + b


def _ln_kernel(x_ref, g_ref, b_ref, o_ref):
    o_ref[...] = _layer_norm_f32(x_ref[...], g_ref[...], b_ref[...]).astype(o_ref.dtype)


def _ln_rows(x, g, b, out_dtype, tr=256):
    t, d = x.shape
    tr = min(tr, t)
    return pl.pallas_call(
        _ln_kernel,
        grid=(t // tr,),
        in_specs=[pl.BlockSpec((tr, d), lambda i: (i, 0)),
                  pl.BlockSpec((1, d), lambda i: (0, 0)),
                  pl.BlockSpec((1, d), lambda i: (0, 0))],
        out_specs=pl.BlockSpec((tr, d), lambda i: (i, 0)),
        out_shape=jax.ShapeDtypeStruct((t, d), out_dtype),
        compiler_params=_params(1),
        name="ln_rows",
    )(x, g.reshape(1, d), b.reshape(1, d))


def _proj_plain_kernel(a_ref, w_ref, o_ref):
    o_ref[...] = jnp.dot(a_ref[...], w_ref[...], preferred_element_type=F32).astype(o_ref.dtype)


def _proj_plain(a, w, out_dtype, tm, tn, name):
    t, k = a.shape
    n = w.shape[1]
    tm, tn = min(tm, t), min(tn, n)
    return pl.pallas_call(
        _proj_plain_kernel,
        grid=(t // tm, n // tn),
        in_specs=[pl.BlockSpec((tm, k), lambda i, j: (i, 0)),
                  pl.BlockSpec((k, tn), lambda i, j: (0, j))],
        out_specs=pl.BlockSpec((tm, tn), lambda i, j: (i, j)),
        out_shape=jax.ShapeDtypeStruct((t, n), out_dtype),
        compiler_params=_params(2),
        name=name,
    )(a, w)


def _proj_rope_kernel(a_ref, w_ref, cos_ref, sin_lo_ref, sin_hi_ref, o_ref, *, n_rot_tiles):
    acc = jnp.dot(a_ref[...], w_ref[...], preferred_element_type=F32)
    j = pl.program_id(1)

    @pl.when(j < n_rot_tiles)
    def _():
        c, s_lo, s_hi = cos_ref[...], sin_lo_ref[...], sin_hi_ref[...]
        half = ROPE_DIMS // 2
        for hh in range(acc.shape[1] // HEAD_DIM):
            a = acc[:, hh * HEAD_DIM:(hh + 1) * HEAD_DIM]
            r = a * c + pltpu.roll(a, HEAD_DIM - half, 1) * s_lo + pltpu.roll(a, half, 1) * s_hi
            o_ref[:, hh * HEAD_DIM:(hh + 1) * HEAD_DIM] = r.astype(o_ref.dtype)

    @pl.when(j >= n_rot_tiles)
    def _():
        o_ref[...] = acc.astype(o_ref.dtype)


def _rope_tables(s):
    half = ROPE_DIMS // 2
    inv_freq = ROPE_THETA ** (-jnp.arange(half, dtype=F32) / half)
    ang = jnp.arange(s, dtype=F32)[:, None] * inv_freq[None, :]
    cos, sin = jnp.cos(ang), jnp.sin(ang)
    pad = HEAD_DIM - ROPE_DIMS
    cos_t = jnp.concatenate([cos, cos, jnp.ones((s, pad), F32)], axis=1)
    zeros = jnp.zeros((s, half), F32)
    sin_lo = jnp.concatenate([-sin, zeros, jnp.zeros((s, pad), F32)], axis=1)
    sin_hi = jnp.concatenate([zeros, sin, jnp.zeros((s, pad), F32)], axis=1)
    return cos_t, sin_lo, sin_hi


def _proj_rope(a, w, tables, seq, n_rot_cols, tm, tn):
    t, k = a.shape
    n = w.shape[1]
    tm = min(tm, seq)
    pos_blocks = seq // tm
    tab_spec = pl.BlockSpec((tm, HEAD_DIM), lambda i, j: (i % pos_blocks, 0))
    return pl.pallas_call(
        functools.partial(_proj_rope_kernel, n_rot_tiles=n_rot_cols // tn),
        grid=(t // tm, n // tn),
        in_specs=[pl.BlockSpec((tm, k), lambda i, j: (i, 0)),
                  pl.BlockSpec((k, tn), lambda i, j: (0, j)),
                  tab_spec, tab_spec, tab_spec],
        out_specs=pl.BlockSpec((tm, tn), lambda i, j: (i, j)),
        out_shape=jax.ShapeDtypeStruct((t, n), BF16),
        compiler_params=_params(2),
        name="proj_rope",
    )(a, w, *tables)


def _proj_gate_kernel(a_ref, w_ref, b_ref, o_ref):
    acc = jnp.dot(a_ref[...], w_ref[...], preferred_element_type=F32)
    o_ref[...] = jax.nn.sigmoid(acc + b_ref[...]).astype(o_ref.dtype)


def _proj_gate(a, w, b, tm, tn):
    t, k = a.shape
    n = w.shape[1]
    tm, tn = min(tm, t), min(tn, n)
    return pl.pallas_call(
        _proj_gate_kernel,
        grid=(t // tm, n // tn),
        in_specs=[pl.BlockSpec((tm, k), lambda i, j: (i, 0)),
                  pl.BlockSpec((k, tn), lambda i, j: (0, j)),
                  pl.BlockSpec((1, tn), lambda i, j: (0, j))],
        out_specs=pl.BlockSpec((tm, tn), lambda i, j: (i, j)),
        out_shape=jax.ShapeDtypeStruct((t, n), BF16),
        compiler_params=_params(2),
        name="proj_gate",
    )(a, w, b.reshape(1, n))


def _na_bias_bands(rel_bias):
    qcol = np.arange(GRID_W)[:, None]
    kcol = np.arange(GRID_W)[None, :]
    win_start = np.clip(qcol - NA_KW // 2, 0, GRID_W - NA_KW)
    valid = (kcol >= win_start) & (kcol < win_start + NA_KW)
    dc = np.clip(kcol - qcol + NA_KW - 1, 0, 2 * NA_KW - 2)
    tab = jnp.take(rel_bias.astype(F32), jnp.asarray(dc), axis=2)
    tab = jnp.where(jnp.asarray(valid)[None, None], tab, NEG_INF)
    rows = np.arange(NA_KH)[:, None] + np.arange(NA_KH)[None, :]
    band = tab[:, rows]
    h = rel_bias.shape[0]
    return band.transpose(0, 1, 3, 2, 4).reshape(h, NA_KH, GRID_W, NA_KH * GRID_W)


def _na_kernel(q_ref, k_ref, v_ref, bias_ref, o_ref, *, rows):
    scale = HEAD_DIM ** -0.5
    band = NA_KH * GRID_W

    def body(r, carry):
        sr = jnp.clip(r - NA_KH // 2, 0, rows - NA_KH)
        q0 = pl.multiple_of(r * GRID_W, GRID_W)
        k0 = pl.multiple_of(sr * GRID_W, GRID_W)
        q = q_ref[pl.ds(q0, GRID_W), :]
        kb = k_ref[pl.ds(k0, band), :]
        vb = v_ref[pl.ds(k0, band), :]
        s = lax.dot_general(q, kb, _NT, preferred_element_type=F32) * scale
        s = s + bias_ref[sr - r + NA_KH - 1]
        m = jnp.max(s, axis=-1, keepdims=True)
        p = jnp.exp(s - m)
        den = jnp.sum(p, axis=-1, keepdims=True)
        o = jnp.dot(p.astype(BF16), vb, preferred_element_type=F32) / den
        o_ref[pl.ds(q0, GRID_W), :] = o.astype(o_ref.dtype)
        return carry

    lax.fori_loop(0, rows, body, 0)


def _na_attention(p_na, bias_bands):
    b, s, _ = p_na.shape
    rows = s // GRID_W
    assert rows >= NA_KH and s % GRID_W == 0
    blk = lambda off: pl.BlockSpec((None, s, HEAD_DIM), lambda bi, h: (bi, 0, off + h))
    return pl.pallas_call(
        functools.partial(_na_kernel, rows=rows),
        grid=(b, NA_HEADS),
        in_specs=[blk(0), blk(NA_HEADS), blk(2 * NA_HEADS),
                  pl.BlockSpec((None, NA_KH, GRID_W, NA_KH * GRID_W), lambda bi, h: (h, 0, 0, 0))],
        out_specs=blk(0),
        out_shape=jax.ShapeDtypeStruct((b, s, NA_HEADS * HEAD_DIM), BF16),
        compiler_params=_params(2),
        name="na_attention",
    )(p_na, p_na, p_na, bias_bands)


def _dil_kernel(q_ref, k_ref, v_ref, o_ref, lse_ref, *, m_len, radius):
    scale = HEAD_DIM ** -0.5
    qb, kw = 2 * radius, 4 * radius

    def body(i, carry):
        q0 = pl.multiple_of(i * qb, qb)
        k0 = pl.multiple_of(jnp.clip(q0 - radius, 0, m_len - kw), radius)
        q = q_ref[pl.ds(q0, qb), :]
        k = k_ref[pl.ds(k0, kw), :]
        v = v_ref[pl.ds(k0, kw), :]
        s = lax.dot_general(q, k, _NT, preferred_element_type=F32) * scale
        mq = q0 + lax.broadcasted_iota(jnp.int32, (qb, kw), 0)
        mk = k0 + lax.broadcasted_iota(jnp.int32, (qb, kw), 1)
        s = jnp.where(jnp.abs(mq - mk) <= radius, s, NEG_INF)
        mx = jnp.max(s, axis=-1, keepdims=True)
        p = jnp.exp(s - mx)
        den = jnp.sum(p, axis=-1, keepdims=True)
        o = jnp.dot(p.astype(BF16), v, preferred_element_type=F32) / den
        o_ref[pl.ds(q0, qb), :] = o
        lse_ref[pl.ds(q0, qb), :] = jnp.broadcast_to(mx + jnp.log(den), (qb, HEAD_DIM))
        return carry

    lax.fori_loop(0, m_len // qb, body, 0)


def _dil_group(p_dl, g, window, dilation):
    b, s, width = p_dl.shape
    m_len = s // dilation
    radius = window // (2 * dilation)
    assert m_len % (2 * radius) == 0 and m_len >= 4 * radius
    lane_blocks = width // HEAD_DIM
    view = p_dl.reshape(b, m_len, dilation * width)
    gh = DIL_GROUP_HEADS
    blk = lambda off: pl.BlockSpec((None, m_len, HEAD_DIM),
                                   lambda bi, rho, j: (bi, 0, rho * lane_blocks + off + g * gh + j))
    out_blk = pl.BlockSpec((None, m_len, HEAD_DIM), lambda bi, rho, j: (bi, 0, rho * gh + j))
    out_sds = jax.ShapeDtypeStruct((b, m_len, dilation * gh * HEAD_DIM), F32)
    o, lse = pl.pallas_call(
        functools.partial(_dil_kernel, m_len=m_len, radius=radius),
        grid=(b, dilation, gh),
        in_specs=[blk(0), blk(DIL_HEADS), blk(2 * DIL_HEADS)],
        out_specs=(out_blk, out_blk),
        out_shape=(out_sds, out_sds),
        compiler_params=_params(3),
        name=f"dil_attention_d{dilation}",
    )(view, view, view)
    return o.reshape(b * s, gh * HEAD_DIM), lse.reshape(b * s, gh * HEAD_DIM)


def _dil_merge_kernel(o0, o1, o2, l0, l1, l2, y_ref):
    a, b, c = l0[...], l1[...], l2[...]
    m = jnp.maximum(jnp.maximum(a, b), c)
    ea, eb, ec = jnp.exp(a - m), jnp.exp(b - m), jnp.exp(c - m)
    num = ea * o0[...] + eb * o1[...] + ec * o2[...]
    y_ref[...] = (num / (ea + eb + ec)).astype(y_ref.dtype)


def _dil_merge(outs, lses, tr=512):
    t, w = outs[0].shape
    tr = min(tr, t)
    spec = pl.BlockSpec((tr, w), lambda i: (i, 0))
    return pl.pallas_call(
        _dil_merge_kernel,
        grid=(t // tr,),
        in_specs=[spec] * 6,
        out_specs=spec,
        out_shape=jax.ShapeDtypeStruct((t, w), BF16),
        compiler_params=_params(1),
        name="dil_merge",
    )(*outs, *lses)


def _branch_kernel(yna_ref, ydl_ref, wna_ref, wdl_ref, gna_ref, gdl_ref, o_ref):
    a = jnp.dot(yna_ref[...], wna_ref[...], preferred_element_type=F32)
    b = jnp.dot(ydl_ref[...], wdl_ref[...], preferred_element_type=F32)
    o_ref[...] = (gna_ref[...].astype(F32) * a + gdl_ref[...].astype(F32) * b).astype(o_ref.dtype)


def _branch_merge(y_na, y_dl, w_na, w_dl, gates, tm, tn):
    t = y_na.shape[0]
    d = w_na.shape[1]
    tm, tn = min(tm, t), min(tn, d)
    gate_blocks = d // tn
    return pl.pallas_call(
        _branch_kernel,
        grid=(t // tm, d // tn),
        in_specs=[pl.BlockSpec((tm, y_na.shape[1]), lambda i, j: (i, 0)),
                  pl.BlockSpec((tm, y_dl.shape[1]), lambda i, j: (i, 0)),
                  pl.BlockSpec((w_na.shape[0], tn), lambda i, j: (0, j)),
                  pl.BlockSpec((w_dl.shape[0], tn), lambda i, j: (0, j)),
                  pl.BlockSpec((tm, tn), lambda i, j: (i, j)),
                  pl.BlockSpec((tm, tn), lambda i, j: (i, gate_blocks + j))],
        out_specs=pl.BlockSpec((tm, tn), lambda i, j: (i, j)),
        out_shape=jax.ShapeDtypeStruct((t, d), BF16),
        compiler_params=_params(2),
        name="branch_merge",
    )(y_na, y_dl, w_na, w_dl, gates, gates)


def _postmix_kernel(x_ref, mix_ref, ge_ref, be_ref, gm_ref, bm_ref, wr_ref, h2_ref, acc_ref, aff_ref):
    h = _layer_norm_f32(x_ref[...], ge_ref[...], be_ref[...])
    h2 = _layer_norm_f32(DEEPNORM_ALPHA * h + mix_ref[...], gm_ref[...], bm_ref[...])
    h2_ref[...] = h2
    acc_ref[...] = DEEPNORM_ALPHA * h2
    logits = lax.dot_general(wr_ref[...], h2, _NT, preferred_element_type=F32,
                             precision=lax.Precision.HIGHEST)
    mx = jnp.max(logits, axis=0, keepdims=True)
    e = jnp.exp(logits - mx)
    aff_ref[...] = e / jnp.sum(e, axis=0, keepdims=True)


def _postmix(x, mix, ge, be, gm, bm, w_router_t, tr=256):
    t, d = x.shape
    e = w_router_t.shape[0]
    tr = min(tr, t)
    row = pl.BlockSpec((tr, d), lambda i: (i, 0))
    vec = pl.BlockSpec((1, d), lambda i: (0, 0))
    return pl.pallas_call(
        _postmix_kernel,
        grid=(t // tr,),
        in_specs=[row, row, vec, vec, vec, vec, pl.BlockSpec((e, d), lambda i: (0, 0))],
        out_specs=(row, row, pl.BlockSpec((e, tr), lambda i: (0, i))),
        out_shape=(jax.ShapeDtypeStruct((t, d), F32), jax.ShapeDtypeStruct((t, d), F32),
                   jax.ShapeDtypeStruct((e, t), F32)),
        compiler_params=_params(1),
        name="postmix_router",
    )(x, mix, ge.reshape(1, d), be.reshape(1, d), gm.reshape(1, d), bm.reshape(1, d), w_router_t)


def _select_kernel(aff_ref, idx_ref, gate_ref, *, cap):
    a = aff_ref[...]
    nc = a.shape[0]
    bits = pltpu.bitcast(a, jnp.int32)
    capf = jnp.float32(cap)

    def count_ge(v):
        c = jnp.where(bits >= v, 1.0, 0.0)
        return jnp.sum(jnp.sum(c, axis=0, keepdims=True), axis=1, keepdims=True)

    def search(_, lohi):
        lo, hi = lohi
        mid = lo + ((hi - lo) >> 1)
        ok = count_ge(mid) >= capf
        return jnp.where(ok, mid, lo), jnp.where(ok, hi, mid)

    lo0 = jnp.zeros((1, 1), jnp.int32)
    hi0 = jnp.full((1, 1), 0x7F800001, jnp.int32)
    thr, _ = lax.fori_loop(0, 31, search, (lo0, hi0))

    def tri(n, rel):
        r = lax.broadcasted_iota(jnp.int32, (n, n), 0)
        c = lax.broadcasted_iota(jnp.int32, (n, n), 1)
        return jnp.where(rel(r, c), 1.0, 0.0).astype(BF16)

    upper_incl = tri(LANES, lambda r, c: r <= c)
    lower_incl = tri(LANES, lambda r, c: c <= r)
    chunk_strict = tri(nc, lambda r, c: c < r)
    chunk_incl = tri(nc, lambda r, c: c <= r)
    eye = tri(LANES, lambda r, c: r == c)
    ones_sq = jnp.ones((LANES, LANES), BF16)

    def mm(x, y):
        return jnp.dot(x, y, preferred_element_type=F32)

    gt = bits > thr
    eq = bits == thr
    n_gt = jnp.sum(jnp.sum(jnp.where(gt, 1.0, 0.0), axis=0, keepdims=True), axis=1, keepdims=True)
    need = capf - n_gt
    eqb = jnp.where(eq, 1.0, 0.0).astype(BF16)
    eq_rank = mm(chunk_strict, mm(eqb, ones_sq).astype(BF16)) + mm(eqb, upper_incl)
    sel = jnp.where(gt, 1.0, jnp.where(eq, jnp.where(eq_rank <= need, 1.0, 0.0), 0.0))
    selb = sel.astype(BF16)

    tot = mm(selb, jnp.ones((LANES, cap), BF16))
    cum_incl = mm(chunk_incl, tot.astype(BF16))
    slot = lax.broadcasted_iota(jnp.int32, (nc, cap), 1).astype(F32)
    chunk_of = jnp.sum(jnp.where(cum_incl <= slot, 1.0, 0.0), axis=0, keepdims=True)
    chunk_id = lax.broadcasted_iota(jnp.int32, (nc, cap), 0).astype(F32)
    onehot = chunk_id == chunk_of
    onehot_b = jnp.where(onehot, 1.0, 0.0).astype(BF16)
    before = jnp.sum(jnp.where(onehot, cum_incl - tot, 0.0), axis=0, keepdims=True)
    rank_in_chunk = slot[0:1, :] - before

    within_t = lax.dot_general(lower_incl, selb, _NT, preferred_element_type=F32)
    w = mm(within_t.astype(BF16), onehot_b)
    lane_of = jnp.sum(jnp.where(w <= rank_in_chunk, 1.0, 0.0), axis=0, keepdims=True)
    idx_ref[...] = (chunk_of * LANES + lane_of).astype(jnp.int32)

    a1 = a.astype(BF16)
    r1 = a - a1.astype(F32)
    a2 = r1.astype(BF16)
    a3 = (r1 - a2.astype(F32)).astype(BF16)
    g = None
    for part in (a1, a2, a3):
        part_t = lax.dot_general(eye, part, _NT, preferred_element_type=F32)
        term = mm(part_t.astype(BF16), onehot_b)
        g = term if g is None else g + term
    lane_id = lax.broadcasted_iota(jnp.int32, (LANES, cap), 0).astype(F32)
    gate_ref[...] = jnp.sum(jnp.where(lane_id == lane_of, g, 0.0), axis=0, keepdims=True)


def _select(aff_t, cap):
    e, t = aff_t.shape
    nc = t // LANES
    out_spec = pl.BlockSpec((None, 1, cap), lambda i: (i, 0, 0))
    idx, gate = pl.pallas_call(
        functools.partial(_select_kernel, cap=cap),
        grid=(e,),
        in_specs=[pl.BlockSpec((None, nc, LANES), lambda i: (i, 0, 0))],
        out_specs=(out_spec, out_spec),
        out_shape=(jax.ShapeDtypeStruct((e, 1, cap), jnp.int32), jax.ShapeDtypeStruct((e, 1, cap), F32)),
        compiler_params=_params(1),
        name="expert_select",
    )(aff_t.reshape(e, nc, LANES))
    return idx.reshape(e, cap), gate.reshape(e, cap)


def _ffn_kernel(idx_ref, gate_ref, h2_hbm, acc_in_hbm, wg_ref, wu_ref, wd_ref, acc_hbm,
                xbuf, ybuf, stage, gsem, ssem, *, tm, chunk):
    del acc_in_hbm
    e, i, f = pl.program_id(0), pl.program_id(1), pl.program_id(2)
    base = i * tm
    n_chunks = tm // chunk

    def row_copy(src_hbm, c, p, sem):
        tok = idx_ref[e, base + c * chunk + p]
        return pltpu.make_async_copy(src_hbm.at[pl.ds(tok, 1), :], stage.at[pl.ds(p, 1), :], sem)

    def gather_chunk(src_hbm, c):
        def start(p, carry):
            row_copy(src_hbm, c, p, gsem).start()
            return carry

        def wait(p, carry):
            row_copy(src_hbm, c, p, gsem).wait()
            return carry

        lax.fori_loop(0, chunk, start, 0)
        lax.fori_loop(0, chunk, wait, 0)

    @pl.when(f == 0)
    def _():
        for c in range(n_chunks):
            gather_chunk(h2_hbm, c)
            xbuf[c * chunk:(c + 1) * chunk, :] = stage[...].astype(BF16)
        ybuf[...] = jnp.zeros_like(ybuf)

    x = xbuf[...]
    hg = jnp.dot(x, wg_ref[...], preferred_element_type=F32)
    hu = jnp.dot(x, wu_ref[...], preferred_element_type=F32)
    hid = (hg * jax.nn.sigmoid(hg) * hu).astype(BF16)
    ybuf[...] += jnp.dot(hid, wd_ref[...], preferred_element_type=F32)

    @pl.when(f == pl.num_programs(2) - 1)
    def _():
        for c in range(n_chunks):
            gather_chunk(acc_hbm, c)
            rows = slice(c * chunk, (c + 1) * chunk)
            stage[...] = stage[...] + ybuf[rows, :] * gate_ref[rows, :]

            def put(p, sem):
                tok = idx_ref[e, base + c * chunk + p]
                return pltpu.make_async_copy(stage.at[pl.ds(p, 1), :], acc_hbm.at[pl.ds(tok, 1), :], sem)

            def start(p, carry):
                put(p, ssem).start()
                return carry

            def wait(p, carry):
                put(p, ssem).wait()
                return carry

            lax.fori_loop(0, chunk, start, 0)
            lax.fori_loop(0, chunk, wait, 0)


def _expert_ffn(idx, gate, h2, acc0, wg, wu, wd, tm=1024, tf=256, chunk=256):
    e, cap = idx.shape
    t, d = h2.shape
    ff = wg.shape[2]
    tm, tf = min(tm, cap), min(tf, ff)
    chunk = min(chunk, tm)
    grid_spec = pltpu.PrefetchScalarGridSpec(
        num_scalar_prefetch=1,
        grid=(e, cap // tm, ff // tf),
        in_specs=[pl.BlockSpec((None, tm, 1), lambda ei, i, f, idx_r: (ei, i, 0)),
                  pl.BlockSpec(memory_space=pl.ANY),
                  pl.BlockSpec(memory_space=pl.ANY),
                  pl.BlockSpec((None, d, tf), lambda ei, i, f, idx_r: (ei, 0, f)),
                  pl.BlockSpec((None, d, tf), lambda ei, i, f, idx_r: (ei, 0, f)),
                  pl.BlockSpec((None, tf, d), lambda ei, i, f, idx_r: (ei, f, 0))],
        out_specs=pl.BlockSpec(memory_space=pl.ANY),
        scratch_shapes=[pltpu.VMEM((tm, d), BF16),
                        pltpu.VMEM((tm, d), F32),
                        pltpu.VMEM((chunk, d), F32),
                        pltpu.SemaphoreType.DMA(()),
                        pltpu.SemaphoreType.DMA(())],
    )
    return pl.pallas_call(
        functools.partial(_ffn_kernel, tm=tm, chunk=chunk),
        grid_spec=grid_spec,
        out_shape=jax.ShapeDtypeStruct((t, d), F32),
        input_output_aliases={3: 0},
        compiler_params=_params(3),
        name="expert_ffn",
    )(idx, gate.reshape(e, cap, 1), h2, acc0, wg, wu, wd)


def _encoder(x, w, consts):
    b, s, d = x.shape
    t = b * s
    xf = x.reshape(t, d)
    h = _ln_rows(xf, w["ln_emb_g"], w["ln_emb_b"], BF16)

    p_na = _proj_plain(h, w["w_in_na"], BF16, 1024, 768, "proj_na")
    p_dl = _proj_rope(h, w["w_in_dl"], consts["rope"][s], s, 2 * DIL_HEADS * HEAD_DIM, 1024, 768)
    gates = _proj_gate(h, w["w_in_gate"], w["b_gate"], 1024, 512)

    y_na = _na_attention(p_na.reshape(b, s, -1), consts["na_bias"]).reshape(t, -1)
    p_dl3 = p_dl.reshape(b, s, -1)
    outs, lses = zip(*[_dil_group(p_dl3, g, win, dil) for g, (win, dil) in enumerate(DIL_PATTERNS)])
    y_dl = _dil_merge(outs, lses)

    merged = _branch_merge(y_na, y_dl, w["w_branch_na"], w["w_branch_dil"], gates, 1024, 512)
    mix = _proj_plain(merged, w["w_out"], F32, 1024, 512, "proj_out")

    h2, acc0, aff_t = _postmix(xf, mix, w["ln_emb_g"], w["ln_emb_b"], w["ln_mix_g"], w["ln_mix_b"],
                               w["w_router_t"])
    cap = (EC_CAPACITY_FACTOR * t) // N_EXPERTS
    idx, gate = _select(aff_t, cap)
    acc = _expert_ffn(idx, gate, h2, acc0, w["w_exp_gate"], w["w_exp_up"], w["w_exp_down"])
    return _ln_rows(acc, w["ln_moe_g"], w["ln_moe_b"], F32).reshape(b, s, d)


def kernel(x_prompt, x_sample, ln_emb_g, ln_emb_b, w_in, b_gate, na_rel_bias, w_branch_na, w_branch_dil, w_out,
           ln_mix_g, ln_mix_b, w_router, w_exp_gate, w_exp_up, w_exp_down, ln_moe_g, ln_moe_b):
    assert w_in.shape[0] == DEPTH
    na_w = 3 * NA_HEADS * HEAD_DIM
    dl_w = 3 * DIL_HEADS * HEAD_DIM
    w_in_b = w_in[0].astype(BF16)
    w = dict(
        ln_emb_g=ln_emb_g, ln_emb_b=ln_emb_b,
        w_in_na=w_in_b[:, :na_w], w_in_dl=w_in_b[:, na_w:na_w + dl_w], w_in_gate=w_in_b[:, na_w + dl_w:],
        b_gate=b_gate[0],
        w_branch_na=w_branch_na[0].astype(BF16), w_branch_dil=w_branch_dil[0].astype(BF16),
        w_out=w_out[0].astype(BF16),
        ln_mix_g=ln_mix_g[0], ln_mix_b=ln_mix_b[0],
        w_router_t=w_router[0].T,
        w_exp_gate=w_exp_gate[0].astype(BF16), w_exp_up=w_exp_up[0].astype(BF16),
        w_exp_down=w_exp_down[0].astype(BF16),
        ln_moe_g=ln_moe_g[0], ln_moe_b=ln_moe_b[0],
    )
    consts = dict(
        na_bias=_na_bias_bands(na_rel_bias[0]),
        rope={s: _rope_tables(s) for s in {x_prompt.shape[1], x_sample.shape[1]}},
    )
    return _encoder(x_prompt, w, consts), _encoder(x_sample, w, consts)
```
